```python
import jax, jax.numpy as jnp
from jax import lax
import numpy as np

D_MODEL = 2048
BATCH = 8
SEQ = 4096
DEPTH = 2
DEC_BATCH = 8
DEC_SEQ = 64
PAST_LEN = 1024

CHUNK = 64
Q_BLOCK = 128
EPS = 1e-6
ROPE_THETA = 10000.0
MLSTM_H = 4
MLSTM_DK = 128
MLSTM_DV = 256
MLA_H = 8
MLA_NOPE = 128
MLA_ROPE = 64
MLA_V = 128
MLA_Q_LORA = 512
MLA_KV_LORA = 512
RET_H = 4
RET_DK = 128
RET_DV = 256
D_FF = ((8 * D_MODEL + 3 * 256 - 1) // (3 * 256)) * 256
IN_SPLITS = (
    MLSTM_H * MLSTM_DK, MLSTM_H * MLSTM_DK, MLSTM_H * MLSTM_DV, MLSTM_H * MLSTM_DV, MLSTM_H, MLSTM_H,
    MLA_Q_LORA, MLA_KV_LORA, MLA_ROPE,
    RET_H * RET_DK, RET_H * RET_DK, RET_H * RET_DV, RET_H * RET_DV,
    D_MODEL, D_MODEL, D_MODEL,
)
D_IN = sum(IN_SPLITS)

kernel_name = 'hybrid_mlstm_mla_retention_stream_step'


def rmsnorm(x, g):
    x32 = x.astype(jnp.float32)
    y = x32 * lax.rsqrt(jnp.mean(x32 * x32, axis=-1, keepdims=True) + EPS)
    return (y * g.astype(jnp.float32)).astype(x.dtype)


def head_norm(x, g, center):
    if center:
        x = x - jnp.mean(x, axis=-1, keepdims=True)
    x = x * lax.rsqrt(jnp.mean(x * x, axis=-1, keepdims=True) + EPS)
    return x.reshape(x.shape[:2] + (-1,)) * g.astype(jnp.float32)


def rope(x, pos):
    d = x.shape[-1]
    inv = 1.0 / (ROPE_THETA ** (jnp.arange(0, d, 2, dtype=jnp.float32) / d))
    ang = pos.astype(jnp.float32)[:, None] * inv[None, :]
    ang = ang.reshape((ang.shape[0],) + (1,) * (x.ndim - 3) + (d // 2,))
    cos, sin = jnp.cos(ang), jnp.sin(ang)
    x32 = x.astype(jnp.float32)
    x1, x2 = x32[..., : d // 2], x32[..., d // 2:]
    return jnp.concatenate([x1 * cos - x2 * sin, x1 * sin + x2 * cos], axis=-1).astype(x.dtype)


def run_chunks(step, carry, xs):
    S = xs[0].shape[1]
    if S <= CHUNK:
        return step(carry, xs)
    nc = S // CHUNK
    xs_c = tuple(jnp.moveaxis(a.reshape((a.shape[0], nc, CHUNK) + a.shape[2:]), 1, 0) for a in xs)
    carry, ys = lax.scan(step, carry, xs_c)
    ys = jnp.moveaxis(ys, 0, 1)
    return carry, ys.reshape((ys.shape[0], S) + ys.shape[3:])


def mlstm_step(carry, xs):
    c, n, m = carry
    q, k, v, ig, lf = xs
    L = q.shape[1]
    b = jnp.cumsum(lf, axis=1).transpose(0, 2, 1)
    i_t = ig.transpose(0, 2, 1)
    causal = jnp.tril(jnp.ones((L, L), dtype=bool))
    log_d = jnp.where(causal, b[..., :, None] - b[..., None, :] + i_t[..., None, :], -jnp.inf)
    log_prev = b + m[..., None]
    m_t = jnp.maximum(log_prev, jnp.max(log_d, axis=-1))
    dmat = jnp.exp(log_d - m_t[..., None])
    prev_scale = jnp.exp(log_prev - m_t)
    w = jnp.einsum('blhd,bshd->bhls', q, k) * dmat
    num = jnp.einsum('bhls,bshv->blhv', w, v) + jnp.einsum('bhl,bhvd,blhd->blhv', prev_scale, c, q)
    den = jnp.sum(w, axis=-1) + prev_scale * jnp.einsum('bhd,blhd->bhl', n, q)
    h = num / jnp.maximum(jnp.abs(den), jnp.exp(-m_t)).transpose(0, 2, 1)[..., None]
    b_last = b[..., -1]
    log_s = b_last[..., None] - b + i_t
    m_new = jnp.maximum(b_last + m, jnp.max(log_s, axis=-1))
    ws = jnp.exp(log_s - m_new[..., None])
    carry_scale = jnp.exp(b_last + m - m_new)
    c_new = carry_scale[..., None, None] * c + jnp.einsum('bhs,bshv,bshd->bhvd', ws, v, k)
    n_new = carry_scale[..., None] * n + jnp.einsum('bhs,bshd->bhd', ws, k)
    return (c_new, n_new, m_new), h


def ret_step(r, xs):
    q, k, v = xs
    L = q.shape[1]
    lg = jnp.log1p(-jnp.exp2(-5.0 - jnp.arange(RET_H, dtype=jnp.float32)))
    idx = jnp.arange(L, dtype=jnp.float32)
    diff = idx[:, None] - idx[None, :]
    dmat = jnp.where(diff >= 0, jnp.exp(jnp.maximum(diff, 0.0)[None] * lg[:, None, None]), 0.0)
    inner = jnp.einsum('blhd,bshd->bhls', q, k) * dmat[None]
    xi = jnp.exp((idx[:, None] + 1.0) * lg[None, :])
    out = jnp.einsum('bhls,bshv->blhv', inner, v) + jnp.einsum('blhd,bhdv->blhv', q, r) * xi[None, :, :, None]
    zeta = jnp.exp((L - 1.0 - idx)[:, None] * lg[None, :])
    r_new = jnp.exp(L * lg)[None, :, None, None] * r + jnp.einsum('bshd,bshv,sh->bhdv', k, v, zeta)
    return r_new, out


def chunk_causal_attention(q, k, v, q_pos, k_pos):
    B, Q, H, _ = q.shape
    Dv = v.shape[-1]
    scale = q.shape[-1] ** -0.5
    k_chunk = k_pos // CHUNK

    def attend(qb, qpb):
        s = jnp.einsum('bqhd,bkhd->bhqk', qb, k).astype(jnp.float32) * scale
        mask = k_chunk[None, :] <= (qpb // CHUNK)[:, None]
        s = jnp.where(mask[None, None], s, -1e30)
        p = jax.nn.softmax(s, axis=-1).astype(v.dtype)
        return jnp.einsum('bhqk,bkhd->bqhd', p, v)

    if Q <= Q_BLOCK:
        return attend(q, q_pos)
    nb = Q // Q_BLOCK
    qb = jnp.moveaxis(q.reshape(B, nb, Q_BLOCK, H, q.shape[-1]), 1, 0)
    pb = q_pos.reshape(nb, Q_BLOCK)
    out = lax.map(lambda a: attend(a[0], a[1]), (qb, pb))
    return jnp.moveaxis(out, 0, 1).reshape(B, Q, H, Dv)


def token_mixers(h, pos0, ckv_past, kr_past, c0, n0, m0, r0, w_in, b_in, g_mlstm, w_up_m,
                 g_qa, w_uq, g_kva, w_ukv, w_up_a, g_ret, w_up_r, w_o):
    f32 = jnp.float32
    B, S, _ = h.shape
    pos = pos0 + jnp.arange(S, dtype=jnp.int32)
    z = h @ w_in + b_in
    split_at = np.cumsum(IN_SPLITS)[:-1].tolist()
    (mq, mk, mv, mo, mi, mf, a_dq, a_dkv, a_kr, rq, rk, rv, rg,
     gate_m, gate_a, gate_r) = jnp.split(z, split_at, axis=-1)

    q = mq.reshape(B, S, MLSTM_H, MLSTM_DK).astype(f32)
    k = mk.reshape(B, S, MLSTM_H, MLSTM_DK).astype(f32) * (MLSTM_DK ** -0.5)
    v = mv.reshape(B, S, MLSTM_H, MLSTM_DV).astype(f32)
    ig = mi.astype(f32)
    lf = jax.nn.log_sigmoid(mf.astype(f32))
    (c1, n1, m1), hm = run_chunks(mlstm_step, (c0.astype(f32), n0.astype(f32), m0.astype(f32)),
                                  (q, k, v, ig, lf))
    ym = head_norm(hm, g_mlstm, False) * jax.nn.sigmoid(mo.astype(f32))
    u_m = ym.astype(h.dtype) @ w_up_m

    qa = (rmsnorm(a_dq, g_qa) @ w_uq).reshape(B, S, MLA_H, MLA_NOPE + MLA_ROPE)
    qa = jnp.concatenate([qa[..., :MLA_NOPE], rope(qa[..., MLA_NOPE:], pos)], axis=-1)
    ckv = rmsnorm(a_dkv, g_kva)
    kr = rope(a_kr, pos)
    ckv_all = jnp.concatenate([ckv_past.astype(h.dtype), ckv], axis=1)
    kr_all = jnp.concatenate([kr_past.astype(h.dtype), kr], axis=1)
    K = ckv_all.shape[1]
    kv = (ckv_all @ w_ukv).reshape(B, K, MLA_H, MLA_NOPE + MLA_V)
    ka = jnp.concatenate([kv[..., :MLA_NOPE],
                          jnp.broadcast_to(kr_all[:, :, None, :], (B, K, MLA_H, MLA_ROPE))], axis=-1)
    ya = chunk_causal_attention(qa, ka, kv[..., MLA_NOPE:], pos, jnp.arange(K, dtype=jnp.int32))
    u_a = ya.reshape(B, S, MLA_H * MLA_V) @ w_up_a

    q = rope(rq.reshape(B, S, RET_H, RET_DK), pos).astype(f32)
    k = rope(rk.reshape(B, S, RET_H, RET_DK), pos).astype(f32) * (RET_DK ** -0.5)
    v = rv.reshape(B, S, RET_H, RET_DV).astype(f32)
    r1, yr = run_chunks(ret_step, r0.astype(f32), (q, k, v))
    yr = head_norm(yr, g_ret, True) * jax.nn.silu(rg.astype(f32))
    u_r = yr.astype(h.dtype) @ w_up_r

    merged = (jax.nn.sigmoid(gate_m) * u_m + jax.nn.sigmoid(gate_a) * u_a
              + jax.nn.sigmoid(gate_r) * u_r)
    return merged @ w_o, (ckv, kr, c1, n1, m1, r1)


def swiglu(h, w_gu, w_down):
    a, g = jnp.split(h @ w_gu, 2, axis=-1)
    return (jax.nn.silu(g) * a) @ w_down


def run_trunk(x, pos0, past_ckv, past_kr, c0, n0, m0, r0, params):
    (g_mix_pre, w_in, b_in, g_mlstm, w_up_m, g_qa, w_uq, g_kva, w_ukv, w_up_a, g_ret, w_up_r,
     w_o, g_mix_post, g_ffn_pre, w_gu, w_down, g_ffn_post) = params
    outs = [[] for _ in range(6)]
    for l in range(DEPTH):
        h = rmsnorm(x, g_mix_pre[l])
        mix, st = token_mixers(h, pos0, past_ckv[l], past_kr[l], c0[l], n0[l], m0[l], r0[l],
                               w_in[l], b_in[l], g_mlstm[l], w_up_m[l], g_qa[l], w_uq[l],
                               g_kva[l], w_ukv[l], w_up_a[l], g_ret[l], w_up_r[l], w_o[l])
        x = x + rmsnorm(mix, g_mix_post[l])
        h = rmsnorm(x, g_ffn_pre[l])
        x = x + rmsnorm(swiglu(h, w_gu[l], w_down[l]), g_ffn_post[l])
        for lst, s in zip(outs, st):
            lst.append(s.astype(x.dtype))
    return x, [jnp.stack(lst) for lst in outs]


def setup_inputs(seed: int = 0) -> dict:
    key = jax.random.key(seed)
    ks = jax.random.split(key, 32)
    f32 = jnp.float32

    def nrm(k, shape, scale):
        return jax.random.normal(k, shape, f32) * scale

    def gain(k, shape):
        return 1.0 + 0.01 * jax.random.normal(k, shape, f32)

    f_off = sum(IN_SPLITS[:5])
    b_in = nrm(ks[10], (DEPTH, D_IN), 0.01)
    b_in = b_in.at[:, f_off:f_off + MLSTM_H].add(jnp.linspace(3.0, 6.0, MLSTM_H))
    return {
        'x_prompt': nrm(ks[0], (BATCH, SEQ, D_MODEL), 1.0),
        'x_sample': nrm(ks[1], (DEC_BATCH, DEC_SEQ, D_MODEL), 1.0),
        'cache_mla_ckv': nrm(ks[2], (DEPTH, DEC_BATCH, PAST_LEN, MLA_KV_LORA), 1.0),
        'cache_mla_krope': nrm(ks[3], (DEPTH, DEC_BATCH, PAST_LEN, MLA_ROPE), 1.0),
        'state_mlstm_c': nrm(ks[4], (DEPTH, DEC_BATCH, MLSTM_H, MLSTM_DV, MLSTM_DK), 1.0),
        'state_mlstm_n': nrm(ks[5], (DEPTH, DEC_BATCH, MLSTM_H, MLSTM_DK), 1.0),
        'state_mlstm_m': nrm(ks[6], (DEPTH, DEC_BATCH, MLSTM_H), 0.5),
        'state_ret': nrm(ks[7], (DEPTH, DEC_BATCH, RET_H, RET_DK, RET_DV), 1.0),
        'g_mix_pre': gain(ks[8], (DEPTH, D_MODEL)),
        'w_in': nrm(ks[9], (DEPTH, D_MODEL, D_IN), D_MODEL ** -0.5),
        'b_in': b_in,
        'g_mlstm': gain(ks[11], (DEPTH, MLSTM_H * MLSTM_DV)),
        'w_up_m': nrm(ks[12], (DEPTH, MLSTM_H * MLSTM_DV, D_MODEL), (MLSTM_H * MLSTM_DV) ** -0.5),
        'g_qa': gain(ks[13], (DEPTH, MLA_Q_LORA)),
        'w_uq': nrm(ks[14], (DEPTH, MLA_Q_LORA, MLA_H * (MLA_NOPE + MLA_ROPE)), MLA_Q_LORA ** -0.5),
        'g_kva': gain(ks[15], (DEPTH, MLA_KV_LORA)),
        'w_ukv': nrm(ks[16], (DEPTH, MLA_KV_LORA, MLA_H * (MLA_NOPE + MLA_V)), MLA_KV_LORA ** -0.5),
        'w_up_a': nrm(ks[17], (DEPTH, MLA_H * MLA_V, D_MODEL), (MLA_H * MLA_V) ** -0.5),
        'g_ret': gain(ks[18], (DEPTH, RET_H * RET_DV)),
        'w_up_r': nrm(ks[19], (DEPTH, RET_H * RET_DV, D_MODEL), (RET_H * RET_DV) ** -0.5),
        'w_o': nrm(ks[20], (DEPTH, D_MODEL, D_MODEL), D_MODEL ** -0.5),
        'g_mix_post': gain(ks[21], (DEPTH, D_MODEL)),
        'g_ffn_pre': gain(ks[22], (DEPTH, D_MODEL)),
        'w_gu': nrm(ks[23], (DEPTH, D_MODEL, 2 * D_FF), D_MODEL ** -0.5),
        'w_down': nrm(ks[24], (DEPTH, D_FF, D_MODEL), D_FF ** -0.5),
        'g_ffn_post': gain(ks[25], (DEPTH, D_MODEL)),
    }


def reference(x_prompt, x_sample, cache_mla_ckv, cache_mla_krope, state_mlstm_c, state_mlstm_n,
              state_mlstm_m, state_ret, g_mix_pre, w_in, b_in, g_mlstm, w_up_m, g_qa, w_uq, g_kva,
              w_ukv, w_up_a, g_ret, w_up_r, w_o, g_mix_post, g_ffn_pre, w_gu, w_down, g_ffn_post):
    params = (g_mix_pre, w_in, b_in, g_mlstm, w_up_m, g_qa, w_uq, g_kva, w_ukv, w_up_a, g_ret,
              w_up_r, w_o, g_mix_post, g_ffn_pre, w_gu, w_down, g_ffn_post)
    dt = x_prompt.dtype
    B = x_prompt.shape[0]
    y_prompt, (p_ckv, p_kr, p_c, p_n, p_m, p_r) = run_trunk(
        x_prompt, 0,
        jnp.zeros((DEPTH, B, 0, MLA_KV_LORA), dt), jnp.zeros((DEPTH, B, 0, MLA_ROPE), dt),
        jnp.zeros((DEPTH, B, MLSTM_H, MLSTM_DV, MLSTM_DK), dt), jnp.zeros((DEPTH, B, MLSTM_H, MLSTM_DK), dt),
        jnp.zeros((DEPTH, B, MLSTM_H), dt), jnp.zeros((DEPTH, B, RET_H, RET_DK, RET_DV), dt),
        params)
    y_sample, (s_ckv, s_kr, s_c, s_n, s_m, s_r) = run_trunk(
        x_sample, cache_mla_ckv.shape[2], cache_mla_ckv, cache_mla_krope,
        state_mlstm_c, state_mlstm_n, state_mlstm_m, state_ret, params)
    return (y_prompt, y_sample, p_ckv, p_kr, p_c, p_n, p_m, p_r,
            s_ckv, s_kr, s_c, s_n, s_m, s_r)
```

```python
import functools
import math

import jax
import jax.numpy as jnp
from jax import lax
from jax.experimental import pallas as pl
from jax.experimental.pallas import tpu as pltpu

F32 = jnp.float32
BF16 = jnp.bfloat16

EPS = 1e-6
ROPE_THETA = 10000.0
MASK_CHUNK = 64
NEG = -1e30

MLSTM_H, MLSTM_DK, MLSTM_DV = 4, 128, 256
MLA_H, MLA_NOPE, MLA_ROPE, MLA_V = 8, 128, 64, 128
MLA_Q_LORA, MLA_KV_LORA = 512, 512
RET_H, RET_DK, RET_DV = 4, 128, 256
MLA_QK_PAD = 256

LANES = 128
VMEM_LIMIT = 56 * 1024 * 1024

Z_MQ, Z_MK, Z_MV, Z_MO = 0, 512, 1024, 2048
Z_ADQ, Z_ADKV = 3072, 3584
Z_RQ, Z_RK, Z_RV, Z_RG = 4096, 4608, 5120, 6144
Z_GATES = 7168
ZS_I, ZS_F = 64, 68


def _cparams(n_axes):
    return pltpu.CompilerParams(dimension_semantics=("arbitrary",) * n_axes,
                                vmem_limit_bytes=VMEM_LIMIT)


def _dot(a, b):
    return jnp.dot(a, b, preferred_element_type=F32)


def _dot_nt(a, b):
    return lax.dot_general(a, b, (((1,), (1,)), ((), ())), preferred_element_type=F32)


def _dot_tn(a, b):
    return lax.dot_general(a, b, (((0,), (0,)), ((), ())), preferred_element_type=F32)


def _split3(x):
    p0 = x.astype(BF16)
    r1 = x - p0.astype(F32)
    p1 = r1.astype(BF16)
    r2 = r1 - p1.astype(F32)
    p2 = r2.astype(BF16)
    return p0, p1, p2


def _sigmoid(x):
    return 1.0 / (1.0 + jnp.exp(-x))


def _log_sigmoid(x):
    return jnp.minimum(x, 0.0) - jnp.log1p(jnp.exp(-jnp.abs(x)))


def _norm_linear_body(*refs, use_norm, has_bias, emit_norm):
    it = iter(refs)
    x_ref = next(it)
    g_ref = next(it) if use_norm else None
    w_ref = next(it)
    b_ref = next(it) if has_bias else None
    o_ref = next(it)
    hn_ref = next(it) if emit_norm else None
    h_scr = next(it)

    @pl.when(pl.program_id(1) == 0)
    def _():
        x = x_ref[...].astype(F32)
        if use_norm:
            ms = jnp.mean(x * x, axis=-1, keepdims=True)
            x = x * lax.rsqrt(ms + EPS) * g_ref[...]
        if emit_norm:
            hn_ref[...] = x
        h_scr[...] = x.astype(BF16)

    acc = _dot(h_scr[...], w_ref[...])
    if has_bias:
        acc = acc + b_ref[...]
    o_ref[...] = acc.astype(o_ref.dtype)


def _pick_tile(n, prefs):
    for t in prefs:
        if t <= n and n % t == 0:
            return t
    return n


def norm_linear(x, g, w, b=None, *, x_col_block=0, out_dtype=BF16, emit_norm=False, name=None):
    M = x.shape[0]
    K, N = w.shape
    use_norm = g is not None
    tm = _pick_tile(M, (1024, 512, 256, 128, 64, 8))
    tn = _pick_tile(N, (1024, 768, 512, 256, 128))
    grid = (M // tm, N // tn)
    in_specs = [pl.BlockSpec((tm, K), lambda i, j: (i, x_col_block))]
    args = [x]
    if use_norm:
        in_specs.append(pl.BlockSpec((1, K), lambda i, j: (0, 0)))
        args.append(g.reshape(1, K).astype(F32))
    in_specs.append(pl.BlockSpec((K, tn), lambda i, j: (0, j)))
    args.append(w)
    if b is not None:
        in_specs.append(pl.BlockSpec((1, tn), lambda i, j: (0, j)))
        args.append(b.reshape(1, N).astype(F32))
    out_shape = [jax.ShapeDtypeStruct((M, N), out_dtype)]
    out_specs = [pl.BlockSpec((tm, tn), lambda i, j: (i, j))]
    if emit_norm:
        out_shape.append(jax.ShapeDtypeStruct((M, K), F32))
        out_specs.append(pl.BlockSpec((tm, K), lambda i, j: (i, 0)))
    body = functools.partial(_norm_linear_body, use_norm=use_norm, has_bias=b is not None,
                             emit_norm=emit_norm)
    outs = pl.pallas_call(
        body, out_shape=out_shape, grid=grid, in_specs=in_specs, out_specs=out_specs,
        scratch_shapes=[pltpu.VMEM((tm, K), BF16)],
        compiler_params=_cparams(2), name=name)(*args)
    return outs if emit_norm else outs[0]


def _mlstm_body(q_ref, k_ref, v_ref, o_ref, gs_ref, c0_ref, n0_ref, m0_ref, g_ref,
                y_ref, c_ref, n_ref, m_ref):
    L = q_ref.shape[0]

    @pl.when(pl.program_id(1) == 0)
    def _():
        c_ref[...] = c0_ref[...]
        n_ref[...] = n0_ref[...]
        m_ref[...] = m0_ref[...]

    gs = gs_ref[...]
    lane = lax.broadcasted_iota(jnp.int32, (L, LANES), 1)
    row = lax.broadcasted_iota(jnp.int32, (L, L), 0)
    col = lax.broadcasted_iota(jnp.int32, (L, L), 1)
    causal = row >= col
    tri = jnp.where(causal, 1.0, 0.0).astype(BF16)
    lf = _log_sigmoid(gs)
    b_all = sum(_dot(tri, p) for p in _split3(lf))
    comb = jnp.where(lane < ZS_F, gs, b_all)
    sel_r = lax.broadcasted_iota(jnp.int32, (8, LANES), 0)
    sel_c = lax.broadcasted_iota(jnp.int32, (8, LANES), 1)
    sel = jnp.where(sel_c == sel_r + ZS_I, 1.0, 0.0).astype(BF16)
    rows = sum(_dot_nt(sel, p) for p in _split3(comb))

    for h in range(MLSTM_H):
        i_row = rows[h:h + 1, :]
        b_row = rows[MLSTM_H + h:MLSTM_H + h + 1, :]
        b_col = b_all[:, ZS_F + h:ZS_F + h + 1]
        i_col = gs[:, ZS_I + h:ZS_I + h + 1]
        m_prev = m_ref[h:h + 1, 0:1]
        log_d = jnp.where(causal, b_col - b_row + i_row, NEG)
        log_prev = b_col + m_prev
        m_t = jnp.maximum(log_prev, jnp.max(log_d, axis=-1, keepdims=True))
        dmat = jnp.exp(log_d - m_t)
        prev_scale = jnp.exp(log_prev - m_t)

        q = q_ref[:, h * MLSTM_DK:(h + 1) * MLSTM_DK]
        kf = k_ref[:, h * MLSTM_DK:(h + 1) * MLSTM_DK].astype(F32) * (MLSTM_DK ** -0.5)
        kb = kf.astype(BF16)
        v = v_ref[:, h * MLSTM_DV:(h + 1) * MLSTM_DV]
        w = _dot_nt(q, kb) * dmat
        cmat = c_ref[h]
        nrow = n_ref[h:h + 1, :]
        num = _dot(w.astype(BF16), v) + prev_scale * _dot_nt(q, cmat.astype(BF16))
        qn = jnp.sum(q.astype(F32) * nrow, axis=-1, keepdims=True)
        den = jnp.sum(w, axis=-1, keepdims=True) + prev_scale * qn
        hm = num / jnp.maximum(jnp.abs(den), jnp.exp(-m_t))

        ms = jnp.mean(hm * hm, axis=-1, keepdims=True)
        yn = hm * lax.rsqrt(ms + EPS) * g_ref[:, h * MLSTM_DV:(h + 1) * MLSTM_DV]
        og = o_ref[:, h * MLSTM_DV:(h + 1) * MLSTM_DV].astype(F32)
        y_ref[:, h * MLSTM_DV:(h + 1) * MLSTM_DV] = (yn * _sigmoid(og)).astype(y_ref.dtype)

        m_new = m_t[L - 1:L, :]
        b_last = b_col[L - 1:L, :]
        ws_col = jnp.exp(b_last - b_col + i_col - m_new)
        carry = prev_scale[L - 1:L, :]
        vs = (v.astype(F32) * ws_col).astype(BF16)
        c_ref[h] = carry * cmat + _dot_tn(vs, kb)
        n_ref[h:h + 1, :] = carry * nrow + jnp.sum(kf * ws_col, axis=0, keepdims=True)
        m_ref[h:h + 1, :] = jnp.broadcast_to(m_new, (1, LANES))


def mlstm_mixer(zb, zs, c0, n0, m0, g, *, chunk):
    B, S, _ = zb.shape
    L = chunk
    nc = S // L
    H, DK, DV = MLSTM_H, MLSTM_DK, MLSTM_DV
    in_specs = [
        pl.BlockSpec((None, L, H * DK), lambda b, c: (b, c, Z_MQ // (H * DK))),
        pl.BlockSpec((None, L, H * DK), lambda b, c: (b, c, Z_MK // (H * DK))),
        pl.BlockSpec((None, L, H * DV), lambda b, c: (b, c, Z_MV // (H * DV))),
        pl.BlockSpec((None, L, H * DV), lambda b, c: (b, c, Z_MO // (H * DV))),
        pl.BlockSpec((None, L, LANES), lambda b, c: (b, c, 0)),
        pl.BlockSpec((None, H, DV, DK), lambda b, c: (b, 0, 0, 0)),
        pl.BlockSpec((None, 8, LANES), lambda b, c: (b, 0, 0)),
        pl.BlockSpec((None, 8, LANES), lambda b, c: (b, 0, 0)),
        pl.BlockSpec((1, H * DV), lambda b, c: (0, 0)),
    ]
    out_shape = [
        jax.ShapeDtypeStruct((B, S, H * DV), BF16),
        jax.ShapeDtypeStruct((B, H, DV, DK), F32),
        jax.ShapeDtypeStruct((B, 8, LANES), F32),
        jax.ShapeDtypeStruct((B, 8, LANES), F32),
    ]
    out_specs = [
        pl.BlockSpec((None, L, H * DV), lambda b, c: (b, c, 0)),
        pl.BlockSpec((None, H, DV, DK), lambda b, c: (b, 0, 0, 0)),
        pl.BlockSpec((None, 8, LANES), lambda b, c: (b, 0, 0)),
        pl.BlockSpec((None, 8, LANES), lambda b, c: (b, 0, 0)),
    ]
    return pl.pallas_call(
        _mlstm_body, out_shape=out_shape, grid=(B, nc), in_specs=in_specs, out_specs=out_specs,
        compiler_params=_cparams(2), name="mlstm_mixer",
    )(zb, zb, zb, zb, zs, c0, n0, m0, g.reshape(1, H * DV).astype(F32))


def _ret_body(q_ref, k_ref, v_ref, rg_ref, cos_ref, sin_ref, r0_ref, g_ref, y_ref, r_ref):
    L = q_ref.shape[0]

    @pl.when(pl.program_id(1) == 0)
    def _():
        r_ref[...] = r0_ref[...]

    cos = cos_ref[...]
    sin = sin_ref[...]
    row = lax.broadcasted_iota(jnp.int32, (L, L), 0)
    col = lax.broadcasted_iota(jnp.int32, (L, L), 1)
    diff = (row - col).astype(F32)
    t_col = lax.broadcasted_iota(jnp.int32, (L, 1), 0).astype(F32)

    for h in range(RET_H):
        lg = math.log1p(-(2.0 ** (-5.0 - h)))
        qf = q_ref[:, h * RET_DK:(h + 1) * RET_DK].astype(F32)
        kf = k_ref[:, h * RET_DK:(h + 1) * RET_DK].astype(F32)
        q = (qf * cos + pltpu.roll(qf, RET_DK // 2, axis=1) * sin).astype(BF16)
        kr = kf * cos + pltpu.roll(kf, RET_DK // 2, axis=1) * sin
        kb = (kr * (RET_DK ** -0.5)).astype(BF16)
        v = v_ref[:, h * RET_DV:(h + 1) * RET_DV]
        dmat = jnp.where(diff >= 0.0, jnp.exp(jnp.maximum(diff, 0.0) * lg), 0.0)
        inner = _dot_nt(q, kb) * dmat
        xi = jnp.exp((t_col + 1.0) * lg)
        zeta = jnp.exp((L - 1.0 - t_col) * lg)
        r = r_ref[h]
        out = _dot(inner.astype(BF16), v) + _dot(q, r.astype(BF16)) * xi
        vz = (v.astype(F32) * zeta).astype(BF16)
        r_ref[h] = math.exp(L * lg) * r + _dot_tn(kb, vz)

        xc = out - jnp.mean(out, axis=-1, keepdims=True)
        ms = jnp.mean(xc * xc, axis=-1, keepdims=True)
        yn = xc * lax.rsqrt(ms + EPS) * g_ref[:, h * RET_DV:(h + 1) * RET_DV]
        rg = rg_ref[:, h * RET_DV:(h + 1) * RET_DV].astype(F32)
        y_ref[:, h * RET_DV:(h + 1) * RET_DV] = (yn * (rg * _sigmoid(rg))).astype(y_ref.dtype)


def ret_mixer(zb, r0, g, cos, sin, *, chunk):
    B, S, _ = zb.shape
    L = chunk
    nc = S // L
    H, DK, DV = RET_H, RET_DK, RET_DV
    in_specs = [
        pl.BlockSpec((None, L, H * DK), lambda b, c: (b, c, Z_RQ // (H * DK))),
        pl.BlockSpec((None, L, H * DK), lambda b, c: (b, c, Z_RK // (H * DK))),
        pl.BlockSpec((None, L, H * DV), lambda b, c: (b, c, Z_RV // (H * DV))),
        pl.BlockSpec((None, L, H * DV), lambda b, c: (b, c, Z_RG // (H * DV))),
        pl.BlockSpec((L, DK), lambda b, c: (c, 0)),
        pl.BlockSpec((L, DK), lambda b, c: (c, 0)),
        pl.BlockSpec((None, H, DK, DV), lambda b, c: (b, 0, 0, 0)),
        pl.BlockSpec((1, H * DV), lambda b, c: (0, 0)),
    ]
    out_shape = [jax.ShapeDtypeStruct((B, S, H * DV), BF16),
                 jax.ShapeDtypeStruct((B, H, DK, DV), F32)]
    out_specs = [pl.BlockSpec((None, L, H * DV), lambda b, c: (b, c, 0)),
                 pl.BlockSpec((None, H, DK, DV), lambda b, c: (b, 0, 0, 0))]
    return pl.pallas_call(
        _ret_body, out_shape=out_shape, grid=(B, nc), in_specs=in_specs, out_specs=out_specs,
        compiler_params=_cparams(2), name="ret_mixer",
    )(zb, zb, zb, zb, cos, sin, r0, g.reshape(1, H * DV).astype(F32))


def _rope64_padded(x, cos, sin):
    lane = lax.broadcasted_iota(jnp.int32, x.shape, 1)
    partner = jnp.where(lane < MLA_ROPE // 2,
                        pltpu.roll(x, LANES - MLA_ROPE // 2, axis=1),
                        pltpu.roll(x, MLA_ROPE // 2, axis=1))
    return x * cos + partner * sin


def _kr_body(zs_ref, cos_ref, sin_ref, kr_ref, krp_ref):
    out = _rope64_padded(zs_ref[...], cos_ref[...], sin_ref[...])
    kr_ref[...] = out[:, :MLA_ROPE]
    krp_ref[...] = out.astype(BF16)


def kr_rope(zs, cos, sin):
    B, S, _ = zs.shape
    ts = _pick_tile(S, (1024, 512, 256, 128, 64))
    return pl.pallas_call(
        _kr_body,
        out_shape=[jax.ShapeDtypeStruct((B, S, MLA_ROPE), F32),
                   jax.ShapeDtypeStruct((B, S, LANES), BF16)],
        grid=(B, S // ts),
        in_specs=[pl.BlockSpec((None, ts, LANES), lambda b, s: (b, s, 0)),
                  pl.BlockSpec((ts, LANES), lambda b, s: (s, 0)),
                  pl.BlockSpec((ts, LANES), lambda b, s: (s, 0))],
        out_specs=[pl.BlockSpec((None, ts, MLA_ROPE), lambda b, s: (b, s, 0)),
                   pl.BlockSpec((None, ts, LANES), lambda b, s: (b, s, 0))],
        compiler_params=_cparams(2), name="kr_rope",
    )(zs, cos, sin)


def _attn_body(qi_tab, ki_tab, last_tab, q_ref, kv_ref, kr_ref, cos_ref, sin_ref, o_ref,
               qs, m_s, l_s, acc, *, pos0, scale):
    p = pl.program_id(1)
    qi = qi_tab[p]
    ki = ki_tab[p]
    tq = q_ref.shape[0]
    tk = kv_ref.shape[0]
    HN = MLA_H * MLA_NOPE

    @pl.when(ki == 0)
    def _():
        cos = cos_ref[...]
        sin = sin_ref[...]
        for h in range(MLA_H):
            base = h * MLA_QK_PAD
            qn = q_ref[:, base:base + MLA_NOPE].astype(F32)
            qr = q_ref[:, base + MLA_NOPE:base + MLA_QK_PAD].astype(F32)
            qs[:, base:base + MLA_NOPE] = (qn * scale).astype(BF16)
            qs[:, base + MLA_NOPE:base + MLA_QK_PAD] = (_rope64_padded(qr, cos, sin) * scale).astype(BF16)
        m_s[...] = jnp.full(m_s.shape, NEG, F32)
        l_s[...] = jnp.zeros(l_s.shape, F32)
        acc[...] = jnp.zeros(acc.shape, F32)

    q_chunk = (pos0 + qi * tq + lax.broadcasted_iota(jnp.int32, (tq, tk), 0)) // MASK_CHUNK
    k_chunk = (ki * tk + lax.broadcasted_iota(jnp.int32, (tq, tk), 1)) // MASK_CHUNK
    bias = jnp.where(k_chunk <= q_chunk, 0.0, NEG)
    krp = kr_ref[...]

    for h in range(MLA_H):
        kfull = jnp.concatenate([kv_ref[:, h * MLA_NOPE:(h + 1) * MLA_NOPE], krp], axis=1)
        s = _dot_nt(qs[:, h * MLA_QK_PAD:(h + 1) * MLA_QK_PAD], kfull) + bias
        m_prev = m_s[h]
        m_new = jnp.maximum(m_prev, jnp.max(s, axis=-1, keepdims=True))
        alpha = jnp.exp(m_prev - m_new)
        pmat = jnp.exp(s - m_new[:, 0:1])
        l_s[h] = alpha * l_s[h] + jnp.sum(pmat, axis=-1, keepdims=True)
        v = kv_ref[:, HN + h * MLA_V:HN + (h + 1) * MLA_V]
        acc[h] = alpha * acc[h] + _dot(pmat.astype(BF16), v)
        m_s[h] = m_new

    @pl.when(last_tab[p] == 1)
    def _():
        for h in range(MLA_H):
            o_ref[:, h * MLA_V:(h + 1) * MLA_V] = (acc[h] / l_s[h]).astype(o_ref.dtype)


def mla_attention(qa, kv, krp, cos, sin, *, pos0, tq, tk):
    B, S, _ = qa.shape
    K = kv.shape[1]
    nq, nk = S // tq, K // tk
    pairs = []
    for i in range(nq):
        last_chunk = (pos0 + (i + 1) * tq - 1) // MASK_CHUNK
        kmax = min(nk - 1, ((last_chunk + 1) * MASK_CHUNK - 1) // tk)
        for j in range(kmax + 1):
            pairs.append((i, j, 1 if j == kmax else 0))
    qi_tab = jnp.array([p[0] for p in pairs], jnp.int32)
    ki_tab = jnp.array([p[1] for p in pairs], jnp.int32)
    last_tab = jnp.array([p[2] for p in pairs], jnp.int32)
    scale = (MLA_NOPE + MLA_ROPE) ** -0.5
    grid_spec = pltpu.PrefetchScalarGridSpec(
        num_scalar_prefetch=3,
        grid=(B, len(pairs)),
        in_specs=[
            pl.BlockSpec((None, tq, MLA_H * MLA_QK_PAD), lambda b, p, qt, kt, lt: (b, qt[p], 0)),
            pl.BlockSpec((None, tk, MLA_H * (MLA_NOPE + MLA_V)), lambda b, p, qt, kt, lt: (b, kt[p], 0)),
            pl.BlockSpec((None, tk, LANES), lambda b, p, qt, kt, lt: (b, kt[p], 0)),
            pl.BlockSpec((tq, LANES), lambda b, p, qt, kt, lt: (qt[p], 0)),
            pl.BlockSpec((tq, LANES), lambda b, p, qt, kt, lt: (qt[p], 0)),
        ],
        out_specs=pl.BlockSpec((None, tq, MLA_H * MLA_V), lambda b, p, qt, kt, lt: (b, qt[p], 0)),
        scratch_shapes=[
            pltpu.VMEM((tq, MLA_H * MLA_QK_PAD), BF16),
            pltpu.VMEM((MLA_H, tq, LANES), F32),
            pltpu.VMEM((MLA_H, tq, LANES), F32),
            pltpu.VMEM((MLA_H, tq, MLA_V), F32),
        ],
    )
    body = functools.partial(_attn_body, pos0=pos0, scale=scale)
    return pl.pallas_call(
        body, out_shape=jax.ShapeDtypeStruct((B, S, MLA_H * MLA_V), BF16), grid_spec=grid_spec,
        compiler_params=_cparams(2), name="mla_attention",
    )(qi_tab, ki_tab, last_tab, qa, kv, krp, cos, sin)


def _merge_body(ym_ref, ya_ref, yr_ref, gm_ref, ga_ref, gr_ref, wm_ref, wa_ref, wr_ref, o_ref):
    out = _sigmoid(gm_ref[...].astype(F32)) * _dot(ym_ref[...], wm_ref[...])
    out = out + _sigmoid(ga_ref[...].astype(F32)) * _dot(ya_ref[...], wa_ref[...])
    out = out + _sigmoid(gr_ref[...].astype(F32)) * _dot(yr_ref[...], wr_ref[...])
    o_ref[...] = out.astype(o_ref.dtype)


def merge_branches(ym, ya, yr, zb, wm, wa, wr):
    M = ym.shape[0]
    D = wm.shape[1]
    tm = _pick_tile(M, (512, 256, 128, 64, 8))
    tn = _pick_tile(D, (1024, 512, 256, 128))
    gate_blk = [(Z_GATES + i * D) // tn for i in range(3)]
    y_spec = [pl.BlockSpec((tm, y.shape[1]), lambda i, j: (i, 0)) for y in (ym, ya, yr)]
    g_spec = [pl.BlockSpec((tm, tn), lambda i, j, o=o: (i, o + j)) for o in gate_blk]
    w_spec = [pl.BlockSpec((w.shape[0], tn), lambda i, j: (0, j)) for w in (wm, wa, wr)]
    return pl.pallas_call(
        _merge_body, out_shape=jax.ShapeDtypeStruct((M, D), BF16), grid=(M // tm, D // tn),
        in_specs=y_spec + g_spec + w_spec,
        out_specs=pl.BlockSpec((tm, tn), lambda i, j: (i, j)),
        compiler_params=_cparams(2), name="merge_branches",
    )(ym, ya, yr, zb, zb, zb, wm, wa, wr)


def _linear_norm_res_body(a_ref, w_ref, g_ref, res_ref, o_ref):
    y = _dot(a_ref[...], w_ref[...])
    ms = jnp.mean(y * y, axis=-1, keepdims=True)
    o_ref[...] = res_ref[...] + y * lax.rsqrt(ms + EPS) * g_ref[...]


def linear_norm_residual(a, w, g, res):
    M, K = a.shape
    D = w.shape[1]
    tm = _pick_tile(M, (512, 256, 128, 64, 8))
    return pl.pallas_call(
        _linear_norm_res_body, out_shape=jax.ShapeDtypeStruct((M, D), F32), grid=(M // tm,),
        in_specs=[pl.BlockSpec((tm, K), lambda i: (i, 0)),
                  pl.BlockSpec((K, D), lambda i: (0, 0)),
                  pl.BlockSpec((1, D), lambda i: (0, 0)),
                  pl.BlockSpec((tm, D), lambda i: (i, 0))],
        out_specs=pl.BlockSpec((tm, D), lambda i: (i, 0)),
        compiler_params=_cparams(1), name="out_proj_norm_res",
    )(a, w, g.reshape(1, D).astype(F32), res)


def _ffn_body(x_ref, gpre_ref, wa_ref, wg_ref, wd_ref, gpost_ref, o_ref, h_scr, acc):
    j = pl.program_id(1)

    @pl.when(j == 0)
    def _():
        x = x_ref[...]
        ms = jnp.mean(x * x, axis=-1, keepdims=True)
        h_scr[...] = (x * lax.rsqrt(ms + EPS) * gpre_ref[...]).astype(BF16)
        acc[...] = jnp.zeros(acc.shape, F32)

    h = h_scr[...]
    a = _dot(h, wa_ref[...])
    gt = _dot(h, wg_ref[...])
    act = (gt * _sigmoid(gt) * a).astype(BF16)
    acc[...] += _dot(act, wd_ref[...])

    @pl.when(j == pl.num_programs(1) - 1)
    def _():
        y = acc[...]
        ms = jnp.mean(y * y, axis=-1, keepdims=True)
        o_ref[...] = x_ref[...] + y * lax.rsqrt(ms + EPS) * gpost_ref[...]


def ffn_block(x, g_pre, w_gu, w_down, g_post):
    M, D = x.shape
    F = w_down.shape[0]
    tm = _pick_tile(M, (512, 256, 128, 64, 8))
    tf = _pick_tile(F, (512, 256, 128))
    nf = F // tf
    return pl.pallas_call(
        _ffn_body, out_shape=jax.ShapeDtypeStruct((M, D), F32), grid=(M // tm, nf),
        in_specs=[pl.BlockSpec((tm, D), lambda i, j: (i, 0)),
                  pl.BlockSpec((1, D), lambda i, j: (0, 0)),
                  pl.BlockSpec((D, tf), lambda i, j: (0, j)),
                  pl.BlockSpec((D, tf), lambda i, j: (0, nf + j)),
                  pl.BlockSpec((tf, D), lambda i, j: (j, 0)),
                  pl.BlockSpec((1, D), lambda i, j: (0, 0))],
        out_specs=pl.BlockSpec((tm, D), lambda i, j: (i, 0)),
        scratch_shapes=[pltpu.VMEM((tm, D), BF16), pltpu.VMEM((tm, D), F32)],
        compiler_params=_cparams(2), name="ffn_block",
    )(x, g_pre.reshape(1, D).astype(F32), w_gu, w_gu, w_down, g_post.reshape(1, D).astype(F32))


def _rope_tables(pos0, S):
    pos = (pos0 + jnp.arange(S, dtype=jnp.int32)).astype(F32)

    def tables(d):
        inv = 1.0 / (ROPE_THETA ** (jnp.arange(0, d, 2, dtype=F32) / d))
        ang = pos[:, None] * inv[None, :]
        cos, sin = jnp.cos(ang), jnp.sin(ang)
        return jnp.concatenate([cos, cos], axis=-1), jnp.concatenate([-sin, sin], axis=-1)

    cos_r, sin_r = tables(RET_DK)
    cos_a, sin_a = tables(MLA_ROPE)
    pad = ((0, 0), (0, LANES - MLA_ROPE))
    return cos_r, sin_r, jnp.pad(cos_a, pad), jnp.pad(sin_a, pad)


def _prep_layer_params(l, D, w_in, b_in, w_up_m, w_uq, w_ukv, w_up_a, w_up_r, w_o, w_gu, w_down):
    sizes = (MLSTM_H * MLSTM_DK, MLSTM_H * MLSTM_DK, MLSTM_H * MLSTM_DV, MLSTM_H * MLSTM_DV,
             MLSTM_H, MLSTM_H, MLA_Q_LORA, MLA_KV_LORA, MLA_ROPE,
             RET_H * RET_DK, RET_H * RET_DK, RET_H * RET_DV, RET_H * RET_DV, D, D, D)
    offs = [0]
    for s in sizes:
        offs.append(offs[-1] + s)
    seg = lambda a, i: a[..., offs[i]:offs[i + 1]]
    big_ids = (0, 1, 2, 3, 6, 7, 9, 10, 11, 12, 13, 14, 15)
    small_ids = (8, 4, 5)
    w, b = w_in[l], b_in[l]
    n_small = sum(sizes[i] for i in small_ids)
    w_big = jnp.concatenate([seg(w, i) for i in big_ids], axis=-1).astype(BF16)
    b_big = jnp.concatenate([seg(b, i) for i in big_ids], axis=-1)
    w_small = jnp.pad(jnp.concatenate([seg(w, i) for i in small_ids], axis=-1),
                      ((0, 0), (0, LANES - n_small))).astype(BF16)
    b_small = jnp.pad(jnp.concatenate([seg(b, i) for i in small_ids], axis=-1), ((0, LANES - n_small),))
    qw = w_uq[l].reshape(MLA_Q_LORA, MLA_H, MLA_NOPE + MLA_ROPE)
    qw = jnp.pad(qw, ((0, 0), (0, 0), (0, MLA_QK_PAD - MLA_NOPE - MLA_ROPE)))
    w_uq_p = qw.reshape(MLA_Q_LORA, MLA_H * MLA_QK_PAD).astype(BF16)
    kvw = w_ukv[l].reshape(MLA_KV_LORA, MLA_H, MLA_NOPE + MLA_V)
    w_ukv_p = jnp.concatenate([kvw[..., :MLA_NOPE].reshape(MLA_KV_LORA, MLA_H * MLA_NOPE),
                               kvw[..., MLA_NOPE:].reshape(MLA_KV_LORA, MLA_H * MLA_V)],
                              axis=-1).astype(BF16)
    return dict(w_big=w_big, b_big=b_big, w_small=w_small, b_small=b_small, w_uq=w_uq_p,
                w_ukv=w_ukv_p, w_up_m=w_up_m[l].astype(BF16), w_up_a=w_up_a[l].astype(BF16),
                w_up_r=w_up_r[l].astype(BF16), w_o=w_o[l].astype(BF16),
                w_gu=w_gu[l].astype(BF16), w_down=w_down[l].astype(BF16))


def _rows8(a):
    B, H = a.shape[:2]
    a = a.reshape(B, H, -1).astype(F32)
    a = jnp.broadcast_to(a, (B, H, LANES)) if a.shape[-1] == 1 else a
    return jnp.pad(a, ((0, 0), (0, 8 - H), (0, 0)))


def _layer(x, pos0, past_ckv, past_kr, c0, n0, m0, r0, P, G, tabs):
    B, S, D = x.shape
    M = B * S
    x2 = x.reshape(M, D)
    cos_r, sin_r, cos_a, sin_a = tabs
    chunk = min(S, 256)

    zb = norm_linear(x2, G["g_mix_pre"], P["w_big"], P["b_big"], name="in_proj")
    zs = norm_linear(x2, G["g_mix_pre"], P["w_small"], P["b_small"], out_dtype=F32, name="in_proj_small")
    zb3 = zb.reshape(B, S, -1)
    zs3 = zs.reshape(B, S, LANES)

    ym, c1, n1, m1 = mlstm_mixer(zb3, zs3, c0.astype(F32), _rows8(n0), _rows8(m0[..., None]),
                                 G["g_mlstm"], chunk=chunk)
    yr, r1 = ret_mixer(zb3, r0.astype(F32), G["g_ret"], cos_r, sin_r, chunk=chunk)

    qa = norm_linear(zb, G["g_qa"], P["w_uq"], x_col_block=Z_ADQ // MLA_Q_LORA, name="q_proj")
    kv, ckv = norm_linear(zb, G["g_kva"], P["w_ukv"], x_col_block=Z_ADKV // MLA_KV_LORA,
                          emit_norm=True, name="kv_proj")
    kr, krp = kr_rope(zs3, cos_a, sin_a)
    kv3 = kv.reshape(B, S, -1)
    if past_ckv is not None:
        Pn = past_ckv.shape[1]
        kv_past = norm_linear(past_ckv.reshape(B * Pn, MLA_KV_LORA).astype(F32), None, P["w_ukv"],
                              name="kv_proj_past")
        krp_past = jnp.pad(past_kr.astype(BF16), ((0, 0), (0, 0), (0, LANES - MLA_ROPE)))
        kv3 = jnp.concatenate([kv_past.reshape(B, Pn, -1), kv3], axis=1)
        krp = jnp.concatenate([krp_past, krp], axis=1)
    K = kv3.shape[1]
    K_pad = -(-K // LANES) * LANES
    if K_pad != K:
        kv3 = jnp.pad(kv3, ((0, 0), (0, K_pad - K), (0, 0)))
        krp = jnp.pad(krp, ((0, 0), (0, K_pad - K), (0, 0)))
    tq = min(S, 256)
    tk = tq if K_pad % tq == 0 and K_pad > 2048 else K_pad
    ya = mla_attention(qa.reshape(B, S, -1), kv3, krp, cos_a, sin_a, pos0=pos0, tq=tq, tk=tk)

    merged = merge_branches(ym.reshape(M, -1), ya.reshape(M, -1), yr.reshape(M, -1), zb,
                            P["w_up_m"], P["w_up_a"], P["w_up_r"])
    x_mid = linear_norm_residual(merged, P["w_o"], G["g_mix_post"], x2)
    x_out = ffn_block(x_mid, G["g_ffn_pre"], P["w_gu"], P["w_down"], G["g_ffn_post"])
    states = (ckv.reshape(B, S, MLA_KV_LORA), kr, c1, n1[:, :MLSTM_H, :], m1[:, :MLSTM_H, 0], r1)
    return x_out.reshape(B, S, D), states


def _trunk(x, pos0, past_ckv, past_kr, c0, n0, m0, r0, params, gains):
    depth = len(params)
    tabs = _rope_tables(pos0, x.shape[1])
    outs = [[] for _ in range(6)]
    for l in range(depth):
        x, st = _layer(x, pos0,
                       None if past_ckv is None else past_ckv[l],
                       None if past_kr is None else past_kr[l],
                       c0[l], n0[l], m0[l], r0[l], params[l], gains[l], tabs)
        for lst, s in zip(outs, st):
            lst.append(s.astype(x.dtype))
    return x, [jnp.stack(lst) for lst in outs]


def kernel(x_prompt, x_sample, cache_mla_ckv, cache_mla_krope, state_mlstm_c, state_mlstm_n, state_mlstm_m, state_ret, g_mix_pre, w_in, b_in, g_mlstm, w_up_m, g_qa, w_uq, g_kva, w_ukv, w_up_a, g_ret, w_up_r, w_o, g_mix_post, g_ffn_pre, w_gu, w_down, g_ffn_post):
    depth = w_in.shape[0]
    D = x_prompt.shape[-1]
    B = x_prompt.shape[0]
    params = [_prep_layer_params(l, D, w_in, b_in, w_up_m, w_uq, w_ukv, w_up_a, w_up_r, w_o, w_gu, w_down)
              for l in range(depth)]
    gains = [dict(g_mix_pre=g_mix_pre[l], g_mlstm=g_mlstm[l], g_qa=g_qa[l], g_kva=g_kva[l],
                  g_ret=g_ret[l], g_mix_post=g_mix_post[l], g_ffn_pre=g_ffn_pre[l],
                  g_ffn_post=g_ffn_post[l]) for l in range(depth)]
    zc = jnp.zeros((depth, B, MLSTM_H, MLSTM_DV, MLSTM_DK), F32)
    zn = jnp.zeros((depth, B, MLSTM_H, MLSTM_DK), F32)
    zm = jnp.zeros((depth, B, MLSTM_H), F32)
    zr = jnp.zeros((depth, B, RET_H, RET_DK, RET_DV), F32)
    y_prompt, (p_ckv, p_kr, p_c, p_n, p_m, p_r) = _trunk(
        x_prompt, 0, None, None, zc, zn, zm, zr, params, gains)
    y_sample, (s_ckv, s_kr, s_c, s_n, s_m, s_r) = _trunk(
        x_sample, cache_mla_ckv.shape[2], cache_mla_ckv, cache_mla_krope,
        state_mlstm_c, state_mlstm_n, state_mlstm_m, state_ret, params, gains)
    return (y_prompt, y_sample, p_ckv, p_kr, p_c, p_n, p_m, p_r,
            s_ckv, s_kr, s_c, s_n, s_m, s_r)
```
